```python
import math
import jax, jax.numpy as jnp
from jax import lax
import numpy as np

D_MODEL = 1024
BATCH = 4
SEQ = 4096
DEPTH = 4
DEC_BATCH = 128
DEC_SEQ = 8
PAST_LEN = 2048
PAGE_SIZE = 128

N_HEADS = 8
HEAD_DIM = 64
D_ATTN = N_HEADS * HEAD_DIM
N_IDX_HEADS = 8
IDX_DIM = 64
TOPK_MAX = 256
D_CONV = D_MODEL // 2
CONV_WIDTH = 31
Q_BLOCK = 128
EPS = 1e-6
D_IN = 4 * D_ATTN + N_IDX_HEADS * IDX_DIM + IDX_DIM + N_IDX_HEADS + 3 * D_CONV + 2 * D_MODEL

kernel_name = "hybrid_dsa_conformer_gated_decoder_step"


def rms_norm(x, g):
    xf = x.astype(jnp.float32)
    y = xf * lax.rsqrt(jnp.mean(xf * xf, axis=-1, keepdims=True) + EPS)
    return (y * g.astype(jnp.float32)).astype(x.dtype)


def layer_norm(x, g, b):
    xf = x.astype(jnp.float32)
    mu = jnp.mean(xf, axis=-1, keepdims=True)
    var = jnp.mean(jnp.square(xf - mu), axis=-1, keepdims=True)
    y = (xf - mu) * lax.rsqrt(var + EPS)
    return (y * g.astype(jnp.float32) + b.astype(jnp.float32)).astype(x.dtype)


def alibi_slopes():
    return jnp.asarray(2.0 ** (-8.0 * np.arange(1, N_HEADS + 1) / N_HEADS), dtype=jnp.float32)


def modulate(x, c, w_ada_l, b_ada_l, g_l):
    mod = jax.nn.silu(c) @ w_ada_l + b_ada_l
    shift, scale, gate = jnp.split(mod, 3, axis=-1)
    h = rms_norm(x, g_l) * (1 + scale[:, None, :]) + shift[:, None, :]
    return h, gate[:, None, :]


def split_projection(p):
    sizes = (D_ATTN, D_ATTN, D_ATTN, D_ATTN, N_IDX_HEADS * IDX_DIM, IDX_DIM, N_IDX_HEADS,
             2 * D_CONV, D_CONV, D_MODEL, D_MODEL)
    parts, start = [], 0
    for s in sizes:
        parts.append(p[..., start:start + s])
        start += s
    return parts


def gather_rows(x, idx):
    return jax.vmap(lambda xb, ib: xb[ib])(x, idx)


def select_keys(qi, wi, ki, q_pos, key_pos, topk):
    rel = jax.nn.relu(jnp.einsum('bqhd,bkd->bqhk', qi, ki).astype(jnp.float32) * (IDX_DIM ** -0.5))
    score = jnp.einsum('bqh,bqhk->bqk', wi.astype(jnp.float32) * (N_IDX_HEADS ** -0.5), rel)
    causal = key_pos[None, :] <= q_pos[:, None]
    score = jnp.where(causal[None], score, -jnp.inf)
    _, idx = lax.top_k(score, topk)
    valid = idx <= q_pos[None, :, None]
    return idx, valid


def sparse_attend(q, k_sel, v_sel, dist, valid, slopes):
    logits = jnp.einsum('bqhd,bqkhd->bqhk', q, k_sel).astype(jnp.float32) * (HEAD_DIM ** -0.5)
    logits = logits - slopes[None, None, :, None] * dist[:, :, None, :].astype(jnp.float32)
    logits = jnp.where(valid[:, :, None, :], logits, -jnp.inf)
    p = jax.nn.softmax(logits, axis=-1).astype(v_sel.dtype)
    return jnp.einsum('bqhk,bqkhd->bqhd', p, v_sel)


def prompt_attention(q, k, v, qi, ki, wi, slopes):
    B, S = q.shape[0], q.shape[1]
    n_blk = S // Q_BLOCK
    topk = min(TOPK_MAX, S // 4)
    key_pos = jnp.arange(S)

    def block(n):
        start = n * Q_BLOCK
        q_b = lax.dynamic_slice_in_dim(q, start, Q_BLOCK, axis=1)
        qi_b = lax.dynamic_slice_in_dim(qi, start, Q_BLOCK, axis=1)
        wi_b = lax.dynamic_slice_in_dim(wi, start, Q_BLOCK, axis=1)
        q_pos = start + jnp.arange(Q_BLOCK)
        idx, valid = select_keys(qi_b, wi_b, ki, q_pos, key_pos, topk)
        k_sel = gather_rows(k, idx)
        v_sel = gather_rows(v, idx)
        return sparse_attend(q_b, k_sel, v_sel, q_pos[None, :, None] - idx, valid, slopes)

    out = lax.map(block, jnp.arange(n_blk))
    return jnp.moveaxis(out, 0, 1).reshape(B, S, D_ATTN)


def sample_attention(l, q, k, v, qi, ki, wi, cache_k, cache_v, cache_kidx, page_table, slopes):
    DB, T = q.shape[0], q.shape[1]
    n_pages = page_table.shape[1]
    L = PAST_LEN + T
    topk = min(TOPK_MAX, L // 4)
    ki_past = cache_kidx[l, page_table].reshape(DB, PAST_LEN, IDX_DIM)
    ki_all = jnp.concatenate([ki_past, ki], axis=1)
    q_pos = PAST_LEN + jnp.arange(T)
    idx, valid = select_keys(qi, wi, ki_all, q_pos, jnp.arange(L), topk)
    in_past = (idx < PAST_LEN)[..., None, None]
    page = jax.vmap(lambda pt, i: pt[i])(page_table, jnp.clip(idx // PAGE_SIZE, 0, n_pages - 1))
    row = idx % PAGE_SIZE
    new_i = jnp.clip(idx - PAST_LEN, 0, T - 1)
    k_sel = jnp.where(in_past, cache_k[l, page, row], gather_rows(k, new_i))
    v_sel = jnp.where(in_past, cache_v[l, page, row], gather_rows(v, new_i))
    out = sparse_attend(q, k_sel, v_sel, q_pos[None, :, None] - idx, valid, slopes)
    return out.reshape(DB, T, D_ATTN)


def conv_branch(ug, zc, hist, b_glu_l, w_dw_l, b_dw_l, g_ln_l, b_ln_l, w_pw2_l):
    a, g = jnp.split(ug + b_glu_l, 2, axis=-1)
    u = a * jax.nn.sigmoid(g)
    u_ext = jnp.concatenate([hist, u], axis=1)
    y = lax.conv_general_dilated(u_ext, w_dw_l[:, None, :].astype(u_ext.dtype), window_strides=(1,),
                                 padding='VALID', dimension_numbers=('NWC', 'WIO', 'NWC'),
                                 feature_group_count=D_CONV) + b_dw_l
    y = jax.nn.silu(layer_norm(y, g_ln_l, b_ln_l)) * jax.nn.silu(zc)
    return y @ w_pw2_l, u_ext[:, -(CONV_WIDTH - 1):]


def setup_inputs(seed: int = 0) -> dict:
    key = jax.random.key(seed)
    ks = jax.random.split(key, 24)
    n_pages = PAST_LEN // PAGE_SIZE
    n_phys = (DEC_BATCH * n_pages * 5) // 4
    f32 = jnp.float32
    nrm = lambda k, shape, s: jax.random.normal(k, shape, f32) * s
    page_table = jax.random.permutation(ks[0], n_phys)[:DEC_BATCH * n_pages].reshape(DEC_BATCH, n_pages).astype(jnp.int32)
    return {
        "x_prompt": nrm(ks[1], (BATCH, SEQ, D_MODEL), 1.0),
        "x_sample": nrm(ks[2], (DEC_BATCH, DEC_SEQ, D_MODEL), 1.0),
        "cache_k": nrm(ks[3], (DEPTH, n_phys, PAGE_SIZE, N_HEADS, HEAD_DIM), 1.0),
        "cache_v": nrm(ks[4], (DEPTH, n_phys, PAGE_SIZE, N_HEADS, HEAD_DIM), 1.0),
        "cache_kidx": nrm(ks[5], (DEPTH, n_phys, PAGE_SIZE, IDX_DIM), 1.0),
        "state_conv": nrm(ks[6], (DEPTH, DEC_BATCH, CONV_WIDTH - 1, D_CONV), 0.5),
        "page_table": page_table,
        "c_prompt": nrm(ks[7], (BATCH, D_MODEL), 1.0),
        "c_sample": nrm(ks[8], (DEC_BATCH, D_MODEL), 1.0),
        "w_ada": nrm(ks[9], (DEPTH, D_MODEL, 3 * D_MODEL), 0.5 * D_MODEL ** -0.5),
        "b_ada": nrm(ks[10], (DEPTH, 3 * D_MODEL), 0.02),
        "g_norm": 1.0 + nrm(ks[11], (DEPTH, D_MODEL), 0.05),
        "w_in": nrm(ks[12], (DEPTH, D_MODEL, D_IN), D_MODEL ** -0.5),
        "b_glu": nrm(ks[13], (DEPTH, 2 * D_CONV), 0.02),
        "w_o_attn": nrm(ks[14], (DEPTH, D_ATTN, D_MODEL), D_ATTN ** -0.5),
        "w_dw": nrm(ks[15], (DEPTH, CONV_WIDTH, D_CONV), CONV_WIDTH ** -0.5),
        "b_dw": nrm(ks[16], (DEPTH, D_CONV), 0.02),
        "g_ln": 1.0 + nrm(ks[17], (DEPTH, D_CONV), 0.05),
        "b_ln": nrm(ks[18], (DEPTH, D_CONV), 0.02),
        "w_pw2": nrm(ks[19], (DEPTH, D_CONV, D_MODEL), D_CONV ** -0.5),
        "w_out": nrm(ks[20], (DEPTH, D_MODEL, D_MODEL), D_MODEL ** -0.5),
        "g_final": 1.0 + nrm(ks[21], (D_MODEL,), 0.05),
    }


def reference(x_prompt, x_sample, cache_k, cache_v, cache_kidx, state_conv, page_table, c_prompt, c_sample,
              w_ada, b_ada, g_norm, w_in, b_glu, w_o_attn, w_dw, b_dw, g_ln, b_ln, w_pw2, w_out, g_final):
    slopes = alibi_slopes()

    def run_layer(l, x, c, hist, attend):
        h, gate = modulate(x, c, w_ada[l], b_ada[l], g_norm[l])
        q, k, v, za, qi, ki, wi, ug, zc, ga, gc = split_projection(h @ w_in[l])
        Bn, T = x.shape[0], x.shape[1]
        q = q.reshape(Bn, T, N_HEADS, HEAD_DIM)
        k = k.reshape(Bn, T, N_HEADS, HEAD_DIM)
        v = v.reshape(Bn, T, N_HEADS, HEAD_DIM)
        qi = qi.reshape(Bn, T, N_IDX_HEADS, IDX_DIM)
        attn = attend(l, q, k, v, qi, ki, wi)
        attn_out = (attn * jax.nn.silu(za)) @ w_o_attn[l]
        conv_out, new_hist = conv_branch(ug, zc, hist, b_glu[l], w_dw[l], b_dw[l], g_ln[l], b_ln[l], w_pw2[l])
        mixed = jax.nn.sigmoid(ga) * attn_out + jax.nn.sigmoid(gc) * conv_out
        x = x + gate * (mixed @ w_out[l])
        return x, k, v, ki, new_hist

    attend_prompt = lambda l, q, k, v, qi, ki, wi: prompt_attention(q, k, v, qi, ki, wi, slopes)
    attend_sample = lambda l, q, k, v, qi, ki, wi: sample_attention(
        l, q, k, v, qi, ki, wi, cache_k, cache_v, cache_kidx, page_table, slopes)

    hp, hs = x_prompt, x_sample
    kp, vp, ip, cp, ks_, vs_, is_, cs_ = [], [], [], [], [], [], [], []
    zero_hist = jnp.zeros((x_prompt.shape[0], CONV_WIDTH - 1, D_CONV), x_prompt.dtype)
    for l in range(DEPTH):
        hp, k1, v1, i1, c1 = run_layer(l, hp, c_prompt, zero_hist, attend_prompt)
        kp.append(k1); vp.append(v1); ip.append(i1); cp.append(c1)
        hs, k2, v2, i2, c2 = run_layer(l, hs, c_sample, state_conv[l].astype(x_sample.dtype), attend_sample)
        ks_.append(k2); vs_.append(v2); is_.append(i2); cs_.append(c2)

    y_prompt = rms_norm(hp, g_final)
    y_sample = rms_norm(hs, g_final)
    return (y_prompt, y_sample,
            jnp.stack(kp), jnp.stack(vp), jnp.stack(ip), jnp.stack(cp),
            jnp.stack(ks_), jnp.stack(vs_), jnp.stack(is_), jnp.stack(cs_))
```

```python
import functools

import jax
import jax.numpy as jnp
import numpy as np
from jax import lax
from jax.experimental import pallas as pl
from jax.experimental.pallas import tpu as pltpu

F32 = jnp.float32
BF16 = jnp.bfloat16
I32 = jnp.int32

N_HEADS = 8
HEAD_DIM = 64
D_ATTN = N_HEADS * HEAD_DIM
N_IDX_HEADS = 8
IDX_DIM = 64
D_IDX = N_IDX_HEADS * IDX_DIM
TOPK_MAX = 256
CONV_WIDTH = 31
HIST = CONV_WIDTH - 1
Q_BLOCK = 128
EPS = 1e-6

LANES = 128
SUBLANES = 8
HIST_PAD = 32
INT_MIN = -(2 ** 31)
NEG_BIG = -1e30
VMEM_LIMIT = 56 * 1024 * 1024

ALIBI_SLOPES = tuple(float(2.0 ** (-8.0 * (h + 1) / N_HEADS)) for h in range(N_HEADS))


def _dot(a, b):
    return jnp.dot(a, b, preferred_element_type=F32)


def _dot_nt(a, b):
    return lax.dot_general(a, b, (((1,), (1,)), ((), ())), preferred_element_type=F32)


def _sigmoid(x):
    return 1.0 / (1.0 + jnp.exp(-x))


def _silu(x):
    return x * _sigmoid(x)


def _sort_key(s):
    bits = pltpu.bitcast(s, I32)
    key = bits ^ ((bits >> 31) & 0x7FFFFFFF)
    return jnp.where(s == 0.0, 0, key)


def _params(sem):
    return pltpu.CompilerParams(dimension_semantics=sem, vmem_limit_bytes=VMEM_LIMIT)


def _ada_kernel(c_ref, w_ref, b_ref, o_ref):
    s = _silu(c_ref[...]).astype(BF16)
    o_ref[0] = _dot(s, w_ref[0].astype(BF16)) + b_ref[0]


def _ada(c_all, w_ada, b_ada):
    depth, d, d3 = w_ada.shape
    rows = c_all.shape[0]
    tn = 1024
    return pl.pallas_call(
        _ada_kernel,
        grid=(depth, d3 // tn),
        in_specs=[
            pl.BlockSpec((rows, d), lambda l, j: (0, 0)),
            pl.BlockSpec((1, d, tn), lambda l, j: (l, 0, j)),
            pl.BlockSpec((1, 1, tn), lambda l, j: (l, 0, j)),
        ],
        out_specs=pl.BlockSpec((1, rows, tn), lambda l, j: (l, 0, j)),
        out_shape=jax.ShapeDtypeStruct((depth, rows, d3), F32),
        compiler_params=_params(("arbitrary", "arbitrary")),
        name="ada",
    )(c_all, w_ada, b_ada.reshape(depth, 1, d3))


def _modulated(x_ref, mod_ref, g_ref):
    x = x_ref[...]
    g, t, d = x.shape
    m = mod_ref[...]
    shift, scale, gate = m[:, :, 0:d], m[:, :, d:2 * d], m[:, :, 2 * d:3 * d]
    xn = x * lax.rsqrt(jnp.mean(x * x, axis=-1, keepdims=True) + EPS) * g_ref[...]
    h = xn * (1.0 + scale) + shift
    return x, h.reshape(g * t, d).astype(BF16), gate


def _pre_prompt_kernel(x_ref, mod_ref, g_ref, wq_ref, wk_ref, wv_ref, wqi_ref, wki_ref, wwi_ref,
                       k_ref, v_ref, ki_ref, kb_ref, kib_ref, qt_ref, vt_ref, qit_ref, lohi_ref):
    _, h, _ = _modulated(x_ref, mod_ref, g_ref)
    k = _dot(h, wk_ref[...])
    k_ref[0] = k
    kb_ref[0] = k.astype(BF16)
    v = _dot(h, wv_ref[...])
    v_ref[0] = v
    vt_ref[0] = v.T.astype(BF16)
    ki = _dot(h, wki_ref[...])[:, :IDX_DIM]
    ki_ref[0] = ki
    kib_ref[0] = ki.astype(BF16)
    q = _dot(h, wq_ref[...]) * (HEAD_DIM ** -0.5)
    qt_ref[0] = q.T.astype(BF16)
    wit = _dot(h, wwi_ref[...]).T[0:N_IDX_HEADS, :] * ((IDX_DIM ** -0.5) * (N_IDX_HEADS ** -0.5))
    qit = _dot(h, wqi_ref[...]).T
    qit_ref[0] = jnp.concatenate(
        [qit[hh * IDX_DIM:(hh + 1) * IDX_DIM, :] * wit[hh:hh + 1, :] for hh in range(N_IDX_HEADS)],
        axis=0).astype(BF16)
    neg = wit < 0.0
    lohi_ref[0] = jnp.concatenate(
        [jnp.where(neg, -jnp.inf, 0.0), jnp.where(neg, 0.0, jnp.inf)], axis=0)


def _pre_prompt(x, mod, g, wts, tm):
    b, s, d = x.shape
    wq, wk, wv, wqi, wki, wwi = wts
    full = lambda a: pl.BlockSpec(a.shape, lambda bi, i: (0,) * a.ndim)
    nat = lambda n: pl.BlockSpec((1, tm, n), lambda bi, i: (bi, i, 0))
    tr = lambda n: pl.BlockSpec((1, n, tm), lambda bi, i: (bi, 0, i))
    return pl.pallas_call(
        _pre_prompt_kernel,
        grid=(b, s // tm),
        in_specs=[nat(d), pl.BlockSpec((1, 1, 3 * d), lambda bi, i: (bi, 0, 0)), full(g),
                  full(wq), full(wk), full(wv), full(wqi), full(wki), full(wwi)],
        out_specs=[nat(D_ATTN), nat(D_ATTN), nat(IDX_DIM), nat(D_ATTN), nat(IDX_DIM),
                   tr(D_ATTN), tr(D_ATTN), tr(D_IDX), tr(2 * N_IDX_HEADS)],
        out_shape=[
            jax.ShapeDtypeStruct((b, s, D_ATTN), F32), jax.ShapeDtypeStruct((b, s, D_ATTN), F32),
            jax.ShapeDtypeStruct((b, s, IDX_DIM), F32),
            jax.ShapeDtypeStruct((b, s, D_ATTN), BF16), jax.ShapeDtypeStruct((b, s, IDX_DIM), BF16),
            jax.ShapeDtypeStruct((b, D_ATTN, s), BF16), jax.ShapeDtypeStruct((b, D_ATTN, s), BF16),
            jax.ShapeDtypeStruct((b, D_IDX, s), BF16), jax.ShapeDtypeStruct((b, 2 * N_IDX_HEADS, s), F32),
        ],
        compiler_params=_params(("arbitrary", "arbitrary")),
        name="pre_prompt",
    )(x, mod, g, wq, wk, wv, wqi, wki, wwi)


def _pre_sample_kernel(x_ref, mod_ref, g_ref, wq_ref, wk_ref, wv_ref, wqi_ref, wki_ref, wwi_ref,
                       k_ref, v_ref, ki_ref, qh_ref, qih_ref, wrow_ref):
    gsz, t, _ = x_ref.shape
    _, h, _ = _modulated(x_ref, mod_ref, g_ref)
    k_ref[...] = _dot(h, wk_ref[...]).reshape(gsz, t, D_ATTN)
    v_ref[...] = _dot(h, wv_ref[...]).reshape(gsz, t, D_ATTN)
    ki_ref[...] = _dot(h, wki_ref[...])[:, :IDX_DIM].reshape(gsz, t, IDX_DIM)
    q = _dot(h, wq_ref[...]) * (HEAD_DIM ** -0.5)
    qi = _dot(h, wqi_ref[...])
    wi = _dot(h, wwi_ref[...]) * ((IDX_DIM ** -0.5) * (N_IDX_HEADS ** -0.5))
    qh, qih, wrow = [], [], []
    for hh in range(N_HEADS):
        wcol = wi[:, hh:hh + 1]
        qh.append(q[:, hh * HEAD_DIM:(hh + 1) * HEAD_DIM].reshape(gsz, t, HEAD_DIM))
        qih.append((qi[:, hh * IDX_DIM:(hh + 1) * IDX_DIM] * wcol).reshape(gsz, t, IDX_DIM))
        wrow.append(jnp.broadcast_to(wcol, (gsz * t, LANES)).reshape(gsz, t, LANES))
    qh_ref[...] = jnp.concatenate(qh, axis=1).astype(BF16)
    qih_ref[...] = jnp.concatenate(qih, axis=1).astype(BF16)
    wrow_ref[...] = jnp.concatenate(wrow, axis=1)


def _pre_sample(x, mod, g, wts, gsz):
    b, t, d = x.shape
    wq, wk, wv, wqi, wki, wwi = wts
    full = lambda a: pl.BlockSpec(a.shape, lambda i: (0,) * a.ndim)
    blk = lambda r, n: pl.BlockSpec((gsz, r, n), lambda i: (i, 0, 0))
    return pl.pallas_call(
        _pre_sample_kernel,
        grid=(b // gsz,),
        in_specs=[blk(t, d), blk(1, 3 * d), full(g),
                  full(wq), full(wk), full(wv), full(wqi), full(wki), full(wwi)],
        out_specs=[blk(t, D_ATTN), blk(t, D_ATTN), blk(t, IDX_DIM),
                   blk(N_HEADS * t, HEAD_DIM), blk(N_IDX_HEADS * t, IDX_DIM), blk(N_IDX_HEADS * t, LANES)],
        out_shape=[
            jax.ShapeDtypeStruct((b, t, D_ATTN), F32), jax.ShapeDtypeStruct((b, t, D_ATTN), F32),
            jax.ShapeDtypeStruct((b, t, IDX_DIM), F32),
            jax.ShapeDtypeStruct((b, N_HEADS * t, HEAD_DIM), BF16),
            jax.ShapeDtypeStruct((b, N_IDX_HEADS * t, IDX_DIM), BF16),
            jax.ShapeDtypeStruct((b, N_IDX_HEADS * t, LANES), F32),
        ],
        compiler_params=_params(("arbitrary",)),
        name="pre_sample",
    )(x, mod, g, wq, wk, wv, wqi, wki, wwi)


def _attn_prompt_kernel(qt_ref, qit_ref, lohi_ref, kib_ref, kb_ref, vt_ref, o_ref,
                        key_scr, qbd_scr, m_scr, l_scr, acc_scr, *, kb, topk):
    n = pl.program_id(1)
    qb = Q_BLOCK
    nch = (n * qb + qb + kb - 1) // kb
    row = lax.broadcasted_iota(I32, (kb, qb), 0)
    q_pos = n * qb + lax.broadcasted_iota(I32, (kb, qb), 1)

    qit = qit_ref[0]
    lohi = lohi_ref[0]
    qi_pairs = [jnp.concatenate([qit[(2 * p) * IDX_DIM:(2 * p + 1) * IDX_DIM, :],
                                 qit[(2 * p + 1) * IDX_DIM:(2 * p + 2) * IDX_DIM, :]], axis=1)
                for p in range(N_IDX_HEADS // 2)]

    def score_chunk(c, carry):
        off = pl.multiple_of(c * kb, kb)
        kic = kib_ref[0, pl.ds(off, kb), :]
        s = jnp.zeros((kb, qb), F32)
        for p in range(N_IDX_HEADS // 2):
            y = _dot(kic, qi_pairs[p])
            for j in range(2):
                hh = 2 * p + j
                yh = y[:, j * qb:(j + 1) * qb]
                s = s + jnp.maximum(jnp.minimum(yh, lohi[N_IDX_HEADS + hh:N_IDX_HEADS + hh + 1, :]),
                                    lohi[hh:hh + 1, :])
        key = jnp.where(off + row <= q_pos, _sort_key(s), INT_MIN)
        key_scr[pl.ds(off, kb), :] = key
        return carry

    lax.fori_loop(0, nch, score_chunk, 0)

    def count(pred):
        def body(c, acc):
            off = pl.multiple_of(c * kb, kb)
            hit = pred(key_scr[pl.ds(off, kb), :], off).astype(I32)
            return acc + hit.reshape(kb // SUBLANES, SUBLANES, qb).sum(axis=0)
        acc = lax.fori_loop(0, nch, body, jnp.zeros((SUBLANES, qb), I32))
        return acc.sum(axis=0, keepdims=True)

    def count_ge(cand):
        return count(lambda kk, off: kk >= cand)

    zero = jnp.zeros((1, qb), I32)
    prefix = jnp.where(count_ge(zero) >= topk, zero, INT_MIN)

    def bit_step(i, prefix):
        cand = prefix | (jnp.int32(1) << (30 - i))
        return jnp.where(count_ge(cand) >= topk, cand, prefix)

    thr = lax.fori_loop(0, 31, bit_step, prefix)
    c_ge = count_ge(thr)

    tied = jnp.logical_and(c_ge > topk, thr > INT_MIN)

    @pl.when(jnp.max(tied.astype(I32)) > 0)
    def _():
        c_gt = count(lambda kk, off: kk > thr)
        need = topk - c_gt
        n_bits = int(np.log2(key_scr.shape[0]))

        def pos_step(i, p):
            cand = p | (jnp.int32(1) << (n_bits - 1 - i))
            cnt = count(lambda kk, off: jnp.logical_and(kk == thr, off + row < cand))
            return jnp.where(cnt < need, cand, p)

        last = lax.fori_loop(0, n_bits, pos_step, zero)

        def demote(c, carry):
            off = pl.multiple_of(c * kb, kb)
            kk = key_scr[pl.ds(off, kb), :]
            drop = jnp.logical_and(kk == thr, off + row > last)
            key_scr[pl.ds(off, kb), :] = jnp.where(drop, INT_MIN, kk)
            return carry

        lax.fori_loop(0, nch, demote, 0)

    thr_sel = jnp.maximum(thr, INT_MIN + 1)

    qt = qt_ref[0]
    hg = LANES * 2 // HEAD_DIM
    for gi in range(N_HEADS // hg):
        rows = []
        for j in range(hg):
            hh = gi * hg + j
            blocks = [jnp.zeros((HEAD_DIM, qb), BF16)] * hg
            blocks[j] = qt[hh * HEAD_DIM:(hh + 1) * HEAD_DIM, :]
            rows.append(jnp.concatenate(blocks, axis=1))
        qbd_scr[gi] = jnp.concatenate(rows, axis=0)
    m_scr[...] = jnp.full(m_scr.shape, NEG_BIG, F32)
    l_scr[...] = jnp.zeros(l_scr.shape, F32)
    acc_scr[...] = jnp.zeros(acc_scr.shape, F32)

    def attn_chunk(c, carry):
        off = pl.multiple_of(c * kb, kb)
        sel = key_scr[pl.ds(off, kb), :] >= thr_sel
        dist = (q_pos - (off + row)).astype(F32)
        for gi in range(N_HEADS // hg):
            kc = kb_ref[0, pl.ds(off, kb), gi * hg * HEAD_DIM:(gi + 1) * hg * HEAD_DIM]
            lg_all = _dot(kc, qbd_scr[gi])
            for j in range(hg):
                hh = gi * hg + j
                lg = lg_all[:, j * qb:(j + 1) * qb] - ALIBI_SLOPES[hh] * dist
                lg = jnp.where(sel, lg, NEG_BIG)
                m_old = m_scr[hh:hh + 1, :]
                m_new = jnp.maximum(m_old, jnp.max(lg, axis=0, keepdims=True))
                p = jnp.where(sel, jnp.exp(lg - m_new), 0.0)
                alpha = jnp.exp(m_old - m_new)
                m_scr[hh:hh + 1, :] = m_new
                l_scr[hh:hh + 1, :] = alpha * l_scr[hh:hh + 1, :] + jnp.sum(p, axis=0, keepdims=True)
                vtc = vt_ref[0, hh * HEAD_DIM:(hh + 1) * HEAD_DIM, pl.ds(off, kb)]
                rs = slice(hh * HEAD_DIM, (hh + 1) * HEAD_DIM)
                acc_scr[rs, :] = alpha * acc_scr[rs, :] + _dot(vtc, p.astype(BF16))
        return carry

    lax.fori_loop(0, nch, attn_chunk, 0)

    out_t = jnp.concatenate(
        [acc_scr[hh * HEAD_DIM:(hh + 1) * HEAD_DIM, :] / l_scr[hh:hh + 1, :] for hh in range(N_HEADS)], axis=0)
    o_ref[0] = out_t.T


def _attn_prompt(qt, qit, lohi, kib, kb_, vt, kb=256):
    b, _, s = qt.shape
    topk = min(TOPK_MAX, s // 4)
    qspec = lambda n: pl.BlockSpec((1, n, Q_BLOCK), lambda bi, i: (bi, 0, i))
    seq = lambda a: pl.BlockSpec((1,) + a.shape[1:], lambda bi, i: (bi, 0, 0))
    return pl.pallas_call(
        functools.partial(_attn_prompt_kernel, kb=kb, topk=topk),
        grid=(b, s // Q_BLOCK),
        in_specs=[qspec(D_ATTN), qspec(D_IDX), qspec(2 * N_IDX_HEADS), seq(kib), seq(kb_), seq(vt)],
        out_specs=pl.BlockSpec((1, Q_BLOCK, D_ATTN), lambda bi, i: (bi, i, 0)),
        out_shape=jax.ShapeDtypeStruct((b, s, D_ATTN), F32),
        scratch_shapes=[
            pltpu.VMEM((s, Q_BLOCK), I32),
            pltpu.VMEM((N_HEADS * HEAD_DIM // (2 * LANES), 2 * LANES, 2 * LANES // HEAD_DIM * Q_BLOCK), BF16),
            pltpu.VMEM((N_HEADS, Q_BLOCK), F32),
            pltpu.VMEM((N_HEADS, Q_BLOCK), F32),
            pltpu.VMEM((D_ATTN, Q_BLOCK), F32),
        ],
        compiler_params=_params(("arbitrary", "arbitrary")),
        name="attn_prompt",
    )(qt, qit, lohi, kib, kb_, vt)


def _score_sample_kernel(pt_ref, qih_ref, wrow_ref, kin_ref, *rest, n_pages, page, t_new, past):
    kidx_refs, key_ref = rest[:n_pages], rest[n_pages]
    qih = qih_ref[0]
    w = wrow_ref[0]
    lo = jnp.where(w < 0.0, -jnp.inf, 0.0)
    hi = jnp.where(w < 0.0, 0.0, jnp.inf)

    def head_sum(y):
        y = jnp.maximum(jnp.minimum(y, hi[:, :y.shape[1]]), lo[:, :y.shape[1]])
        s = y[0:t_new]
        for hh in range(1, N_IDX_HEADS):
            s = s + y[hh * t_new:(hh + 1) * t_new]
        return s

    for j in range(n_pages):
        kp = kidx_refs[j][0, 0].astype(BF16)
        key_ref[0, :, j * page:(j + 1) * page] = _sort_key(head_sum(_dot_nt(qih, kp)))
    kn = jnp.concatenate([kin_ref[0], jnp.zeros((page - t_new, IDX_DIM), F32)], axis=0).astype(BF16)
    s_new = head_sum(_dot_nt(qih, kn))
    tq = lax.broadcasted_iota(I32, (t_new, page), 0)
    tk = lax.broadcasted_iota(I32, (t_new, page), 1)
    key_ref[0, :, n_pages * page:(n_pages + 1) * page] = jnp.where(tk <= tq, _sort_key(s_new), INT_MIN)


def _score_sample(l, page_table, qih, wrow, ki_new, cache_kidx):
    b, n_pages = page_table.shape
    page = cache_kidx.shape[2]
    t_new = ki_new.shape[1]
    lk = (n_pages + 1) * page
    seqspec = lambda a: pl.BlockSpec((1,) + a.shape[1:], lambda i, pt: (i, 0, 0))
    page_spec = lambda j: pl.BlockSpec((1, 1, page, IDX_DIM), lambda i, pt: (l, pt[i, j], 0, 0))
    return pl.pallas_call(
        functools.partial(_score_sample_kernel, n_pages=n_pages, page=page, t_new=t_new,
                          past=n_pages * page),
        grid_spec=pltpu.PrefetchScalarGridSpec(
            num_scalar_prefetch=1,
            grid=(b,),
            in_specs=[seqspec(qih), seqspec(wrow), seqspec(ki_new)] + [page_spec(j) for j in range(n_pages)],
            out_specs=pl.BlockSpec((1, t_new, lk), lambda i, pt: (i, 0, 0)),
        ),
        out_shape=jax.ShapeDtypeStruct((b, t_new, lk), I32),
        compiler_params=_params(("arbitrary",)),
        name="score_sample",
    )(page_table, qih, wrow, ki_new, *([cache_kidx] * n_pages))


def _select_sample_kernel(key_ref, sel_ref, *, topk):
    keys = key_ref[...]
    g, t, lk = keys.shape
    keys = keys.reshape(g * t, lk)
    pos = lax.broadcasted_iota(I32, keys.shape, 1)

    def count(hit):
        return jnp.sum(hit.astype(I32), axis=1, keepdims=True)

    zero = jnp.zeros((g * t, 1), I32)
    prefix = jnp.where(count(keys >= zero) >= topk, zero, INT_MIN)

    def bit_step(i, prefix):
        cand = prefix | (jnp.int32(1) << (30 - i))
        return jnp.where(count(keys >= cand) >= topk, cand, prefix)

    thr = lax.fori_loop(0, 31, bit_step, prefix)
    need = topk - count(keys > thr)
    eq = keys == thr
    n_bits = int(np.ceil(np.log2(lk)))

    def pos_step(i, p):
        cand = p | (jnp.int32(1) << (n_bits - 1 - i))
        cnt = count(jnp.logical_and(eq, pos < cand))
        return jnp.where(cnt < need, cand, p)

    last = lax.fori_loop(0, n_bits, pos_step, zero)
    sel = jnp.logical_and(jnp.logical_or(keys > thr, jnp.logical_and(eq, pos <= last)), keys > INT_MIN)
    sel_ref[...] = jnp.where(sel, 1.0, 0.0).reshape(g, t, lk)


def _select_sample(keys, topk, gsz=16):
    b, t, lk = keys.shape
    spec = pl.BlockSpec((gsz, t, lk), lambda i: (i, 0, 0))
    return pl.pallas_call(
        functools.partial(_select_sample_kernel, topk=topk),
        grid=(b // gsz,),
        in_specs=[spec],
        out_specs=spec,
        out_shape=jax.ShapeDtypeStruct((b, t, lk), F32),
        compiler_params=_params(("arbitrary",)),
        name="select_sample",
    )(keys)


def _attn_sample_kernel(pt_ref, qh_ref, sel_ref, kn_ref, vn_ref, *rest, n_pages, page, t_new, past):
    k_refs, v_refs = rest[:n_pages], rest[n_pages:2 * n_pages]
    o_ref, lg_scr = rest[2 * n_pages], rest[2 * n_pages + 1]
    rows = N_HEADS * t_new
    cols = page * N_HEADS
    qh = qh_ref[0]
    sel = sel_ref[0].astype(BF16)

    def consts(width):
        r_i = lax.broadcasted_iota(I32, (rows, width), 0)
        c_i = lax.broadcasted_iota(I32, (rows, width), 1)
        head_ok = (c_i % N_HEADS) == (r_i // t_new)
        slope = jnp.zeros((rows, width), F32)
        for hh in range(N_HEADS):
            slope = jnp.where((r_i // t_new) == hh, ALIBI_SLOPES[hh], slope)
        bias = slope * (past + (r_i % t_new) - c_i // N_HEADS).astype(F32)
        return head_ok, slope, bias

    e_r = lax.broadcasted_iota(I32, (page, cols), 0)
    e_c = lax.broadcasted_iota(I32, (page, cols), 1)
    expand = jnp.where(e_c // N_HEADS == e_r, 1.0, 0.0).astype(BF16)

    def masked_logits(k2, sel_page, base, cst):
        head_ok, slope, bias = cst
        width = head_ok.shape[1]
        lg = _dot_nt(qh, k2)
        se = _dot(sel_page, expand[:, :width])
        se = jnp.concatenate([se] * N_HEADS, axis=0)
        ok = jnp.logical_and(se > 0.5, head_ok)
        return jnp.where(ok, lg - (bias - slope * float(base)), NEG_BIG), ok

    cst_page = consts(cols)
    m = jnp.full((rows, 1), NEG_BIG, F32)
    for j in range(n_pages):
        k2 = k_refs[j][0, 0].reshape(cols, HEAD_DIM).astype(BF16)
        lg, _ = masked_logits(k2, sel[:, j * page:(j + 1) * page], j * page, cst_page)
        lg_scr[:, j * cols:(j + 1) * cols] = lg
        m = jnp.maximum(m, jnp.max(lg, axis=1, keepdims=True))
    wn = t_new * N_HEADS
    kn2 = kn_ref[0].reshape(wn, HEAD_DIM).astype(BF16)
    lg_n, ok_n = masked_logits(kn2, sel[:, n_pages * page:(n_pages + 1) * page], past, consts(wn))
    m = jnp.maximum(m, jnp.max(lg_n, axis=1, keepdims=True))

    p_n = jnp.where(ok_n, jnp.exp(lg_n - m), 0.0)
    denom = jnp.sum(p_n, axis=1, keepdims=True)
    acc = _dot(p_n.astype(BF16), vn_ref[0].reshape(wn, HEAD_DIM).astype(BF16))
    for j in range(n_pages):
        lg = lg_scr[:, j * cols:(j + 1) * cols]
        p = jnp.where(lg > 0.5 * NEG_BIG, jnp.exp(lg - m), 0.0)
        denom = denom + jnp.sum(p, axis=1, keepdims=True)
        v2 = v_refs[j][0, 0].reshape(cols, HEAD_DIM).astype(BF16)
        acc = acc + _dot(p.astype(BF16), v2)
    o_ref[0] = acc / denom


def _attn_sample(l, page_table, qh, sel, k_new, v_new, cache_k, cache_v):
    b, n_pages = page_table.shape
    page = cache_k.shape[2]
    t_new = k_new.shape[1]
    rows = N_HEADS * t_new
    seqspec = lambda a: pl.BlockSpec((1,) + a.shape[1:], lambda i, pt: (i,) + (0,) * (a.ndim - 1))
    page_spec = lambda j: pl.BlockSpec((1, 1, page, N_HEADS, HEAD_DIM), lambda i, pt: (l, pt[i, j], 0, 0, 0))
    return pl.pallas_call(
        functools.partial(_attn_sample_kernel, n_pages=n_pages, page=page, t_new=t_new,
                          past=n_pages * page),
        grid_spec=pltpu.PrefetchScalarGridSpec(
            num_scalar_prefetch=1,
            grid=(b,),
            in_specs=[seqspec(qh), seqspec(sel), seqspec(k_new), seqspec(v_new)]
                     + [page_spec(j) for j in range(n_pages)] * 2,
            out_specs=pl.BlockSpec((1, rows, HEAD_DIM), lambda i, pt: (i, 0, 0)),
            scratch_shapes=[pltpu.VMEM((rows, n_pages * page * N_HEADS), F32)],
        ),
        out_shape=jax.ShapeDtypeStruct((b, rows, HEAD_DIM), F32),
        compiler_params=_params(("arbitrary",)),
        name="attn_sample",
    )(page_table, qh, sel, k_new, v_new, *([cache_k] * n_pages), *([cache_v] * n_pages))


def _post_tail(x, h, gate, attn, u_conv, refs, final):
    (wza, wzc, wga, wgc, wo, bdw, gln, bln, wpw2, wout, gfin) = refs
    g, t, d = x.shape
    y = u_conv + bdw[...]
    mu = jnp.mean(y, axis=-1, keepdims=True)
    yc = y - mu
    var = jnp.mean(yc * yc, axis=-1, keepdims=True)
    yn = yc * lax.rsqrt(var + EPS) * gln[...] + bln[...]
    conv_in = _silu(yn) * _silu(_dot(h, wzc[...]))
    conv_out = _dot(conv_in.astype(BF16), wpw2[...])
    attn_out = _dot((attn * _silu(_dot(h, wza[...]))).astype(BF16), wo[...])
    mixed = _sigmoid(_dot(h, wga[...])) * attn_out + _sigmoid(_dot(h, wgc[...])) * conv_out
    xo = x + gate * _dot(mixed.astype(BF16), wout[...]).reshape(g, t, d)
    if final:
        xo = xo * lax.rsqrt(jnp.mean(xo * xo, axis=-1, keepdims=True) + EPS) * gfin[...]
    return xo


def _glu(h, wug, bglu):
    ug = _dot(h, wug[...]) + bglu[...]
    dc = ug.shape[1] // 2
    return ug[:, :dc] * _sigmoid(ug[:, dc:])


def _post_prompt_kernel(x_ref, mod_ref, g_ref, attn_ref, wug, bglu, wdw, wza, wzc, wga, wgc, wo, bdw, gln, bln,
                        wpw2, wout, gfin, xo_ref, hist_ref, u_scr, *, final):
    i = pl.program_id(1)
    tm = x_ref.shape[1]
    x, h, gate = _modulated(x_ref, mod_ref, g_ref)

    @pl.when(i == 0)
    def _():
        u_scr[0:HIST_PAD, :] = jnp.zeros((HIST_PAD, u_scr.shape[1]), F32)

    u_scr[HIST_PAD:HIST_PAD + tm, :] = _glu(h, wug, bglu)
    base = HIST_PAD - HIST
    conv = u_scr[base:base + tm, :] * wdw[0:1, :]
    for w in range(1, CONV_WIDTH):
        conv = conv + u_scr[base + w:base + w + tm, :] * wdw[w:w + 1, :]
    hist_ref[0] = u_scr[tm + base:tm + HIST_PAD, :]
    u_scr[0:HIST_PAD, :] = u_scr[tm:tm + HIST_PAD, :]
    xo_ref[...] = _post_tail(x, h, gate, attn_ref[0], conv,
                             (wza, wzc, wga, wgc, wo, bdw, gln, bln, wpw2, wout, gfin), final)


def _post_sample_kernel(x_ref, mod_ref, g_ref, attn_ref, hin_ref, wug, bglu, wdw, wza, wzc, wga, wgc, wo, bdw,
                        gln, bln, wpw2, wout, gfin, xo_ref, hist_ref, u_scr, *, final):
    gsz, t, _ = x_ref.shape
    dc = u_scr.shape[2]
    x, h, gate = _modulated(x_ref, mod_ref, g_ref)
    base = HIST_PAD - HIST
    u_scr[:, base:HIST_PAD, :] = hin_ref[...]
    u_scr[:, HIST_PAD:HIST_PAD + t, :] = _glu(h, wug, bglu).reshape(gsz, t, dc)
    conv = u_scr[:, base:base + t, :] * wdw[0:1, :]
    for w in range(1, CONV_WIDTH):
        conv = conv + u_scr[:, base + w:base + w + t, :] * wdw[w:w + 1, :]
    hist_ref[...] = u_scr[:, base + t:HIST_PAD + t, :]
    xo_ref[...] = _post_tail(x, h, gate, attn_ref[...].reshape(gsz * t, D_ATTN), conv.reshape(gsz * t, dc),
                             (wza, wzc, wga, wgc, wo, bdw, gln, bln, wpw2, wout, gfin), final)


def _post_prompt(x, mod, g, attn, wts, tm, final):
    b, s, d = x.shape
    dc = wts[2].shape[1]
    full = lambda a: pl.BlockSpec(a.shape, lambda bi, i: (0,) * a.ndim)
    nat = lambda n: pl.BlockSpec((1, tm, n), lambda bi, i: (bi, i, 0))
    return pl.pallas_call(
        functools.partial(_post_prompt_kernel, final=final),
        grid=(b, s // tm),
        in_specs=[nat(d), pl.BlockSpec((1, 1, 3 * d), lambda bi, i: (bi, 0, 0)), full(g), nat(D_ATTN)]
                 + [full(w) for w in wts],
        out_specs=[nat(d), pl.BlockSpec((1, HIST, dc), lambda bi, i: (bi, 0, 0))],
        out_shape=[jax.ShapeDtypeStruct((b, s, d), F32), jax.ShapeDtypeStruct((b, HIST, dc), F32)],
        scratch_shapes=[pltpu.VMEM((tm + HIST_PAD, dc), F32)],
        compiler_params=_params(("arbitrary", "arbitrary")),
        name="post_prompt",
    )(x, mod, g, attn, *wts)


def _post_sample(x, mod, g, attn, hist, wts, gsz, final):
    b, t, d = x.shape
    dc = wts[2].shape[1]
    full = lambda a: pl.BlockSpec(a.shape, lambda i: (0,) * a.ndim)
    blk = lambda r, n: pl.BlockSpec((gsz, r, n), lambda i: (i, 0, 0))
    return pl.pallas_call(
        functools.partial(_post_sample_kernel, final=final),
        grid=(b // gsz,),
        in_specs=[blk(t, d), blk(1, 3 * d), full(g), blk(t, D_ATTN), blk(HIST, dc)] + [full(w) for w in wts],
        out_specs=[blk(t, d), blk(HIST, dc)],
        out_shape=[jax.ShapeDtypeStruct((b, t, d), F32), jax.ShapeDtypeStruct((b, HIST, dc), F32)],
        scratch_shapes=[pltpu.VMEM((gsz, HIST_PAD + t, dc), F32)],
        compiler_params=_params(("arbitrary",)),
        name="post_sample",
    )(x, mod, g, attn, hist, *wts)


def kernel(x_prompt, x_sample, cache_k, cache_v, cache_kidx, state_conv, page_table, c_prompt, c_sample,
           w_ada, b_ada, g_norm, w_in, b_glu, w_o_attn, w_dw, b_dw, g_ln, b_ln, w_pw2, w_out, g_final):
    depth, d, _ = w_ada.shape
    bp, s, _ = x_prompt.shape
    bs, t_new, _ = x_sample.shape
    dc = w_dw.shape[2]
    n_pages = page_table.shape[1]
    page = cache_k.shape[2]
    past = n_pages * page
    topk_s = min(TOPK_MAX, (past + t_new) // 4)

    pad = (-bp) % SUBLANES
    c_all = jnp.concatenate([c_prompt, jnp.zeros((pad, d), F32), c_sample], axis=0)
    mod = _ada(c_all, w_ada, b_ada)
    mod_p = mod[:, :bp].reshape(depth, bp, 1, 3 * d)
    mod_s = mod[:, bp + pad:].reshape(depth, bs, 1, 3 * d)

    sizes = (D_ATTN, D_ATTN, D_ATTN, D_ATTN, D_IDX, IDX_DIM, N_IDX_HEADS, 2 * dc, dc, d, d)
    offs = np.concatenate([[0], np.cumsum(sizes)])
    col = lambda i: w_in[:, :, offs[i]:offs[i + 1]].astype(BF16)
    lane_pad = lambda w: jnp.pad(w, ((0, 0), (0, 0), (0, LANES - w.shape[2])))
    wq, wk, wv, wza, wqi = col(0), col(1), col(2), col(3), col(4)
    wki, wwi = lane_pad(col(5)), lane_pad(col(6))
    wug, wzc, wga, wgc = col(7), col(8), col(9), col(10)
    wo, wpw2, wout = w_o_attn.astype(BF16), w_pw2.astype(BF16), w_out.astype(BF16)
    row = lambda a: a.reshape(depth, 1, a.shape[-1])
    g_norm_r, b_glu_r, b_dw_r, g_ln_r, b_ln_r = row(g_norm), row(b_glu), row(b_dw), row(g_ln), row(b_ln)
    g_fin = g_final.reshape(1, d)

    hp, hs = x_prompt, x_sample
    outs = [[] for _ in range(8)]
    for l in range(depth):
        final = l == depth - 1
        pre_w = (wq[l], wk[l], wv[l], wqi[l], wki[l], wwi[l])
        post_w = (wug[l], b_glu_r[l], w_dw[l], wza[l], wzc[l], wga[l], wgc[l], wo[l], b_dw_r[l], g_ln_r[l],
                  b_ln_r[l], wpw2[l], wout[l], g_fin)

        k1, v1, i1, kb1, kib1, qt1, vt1, qit1, lohi1 = _pre_prompt(hp, mod_p[l], g_norm_r[l], pre_w, tm=512)
        attn_p = _attn_prompt(qt1, qit1, lohi1, kib1, kb1, vt1)
        hp, c1 = _post_prompt(hp, mod_p[l], g_norm_r[l], attn_p, post_w, tm=256, final=final)

        k2, v2, i2, qh2, qih2, wrow2 = _pre_sample(hs, mod_s[l], g_norm_r[l], pre_w, gsz=16)
        k2 = k2.reshape(bs, t_new, N_HEADS, HEAD_DIM)
        v2 = v2.reshape(bs, t_new, N_HEADS, HEAD_DIM)
        keys = _score_sample(l, page_table, qih2, wrow2, i2, cache_kidx)
        sel = _select_sample(keys, topk_s)
        attn_s = _attn_sample(l, page_table, qh2, sel, k2, v2, cache_k, cache_v)
        attn_s = attn_s.reshape(bs, N_HEADS, t_new, HEAD_DIM).transpose(0, 2, 1, 3).reshape(bs, t_new, D_ATTN)
        hs, c2 = _post_sample(hs, mod_s[l], g_norm_r[l], attn_s, state_conv[l], post_w, gsz=16, final=final)

        for lst, val in zip(outs, (k1.reshape(bp, s, N_HEADS, HEAD_DIM), v1.reshape(bp, s, N_HEADS, HEAD_DIM),
                                   i1, c1, k2, v2, i2, c2)):
            lst.append(val)

    return (hp, hs) + tuple(jnp.stack(o) for o in outs)
```
